```python
import math
import jax, jax.numpy as jnp
from jax import lax
import numpy as np

D_MODEL = 2048
BATCH = 1
SEQ = 8192
DEPTH = 4

DIFF_HEADS = 8
DIFF_QK_DIM = 64
DIFF_V_DIM = 128
DIFF_WIDTH = DIFF_HEADS * DIFF_V_DIM
MLA_HEADS = 8
MLA_Q_RANK = 512
MLA_KV_RANK = 512
MLA_NOPE_DIM = 128
MLA_ROPE_DIM = 64
MLA_V_DIM = 128
MLA_WIDTH = MLA_HEADS * MLA_V_DIM
MIX_WIDTH = DIFF_WIDTH + MLA_WIDTH
IN_SIZES = (
    DIFF_HEADS * 2 * DIFF_QK_DIM,
    DIFF_HEADS * 2 * DIFF_QK_DIM,
    DIFF_WIDTH,
    DIFF_WIDTH,
    MLA_Q_RANK,
    MLA_KV_RANK,
    MLA_ROPE_DIM,
    MLA_WIDTH,
)
IN_WIDTH = sum(IN_SIZES)
ROPE_THETA = 10000.0
NUM_BUCKETS = 32
MAX_DISTANCE = 128
BLOCK_Q = 128
EPS = 1e-6
NEG_INF = -1e30

kernel_name = "hymba_diffattn_mla_hybrid"


def _rms_norm(x, g):
    xf = x.astype(jnp.float32)
    y = xf * lax.rsqrt(jnp.mean(xf * xf, axis=-1, keepdims=True) + EPS)
    return (y * g.astype(jnp.float32)).astype(x.dtype)


def _rope_tables(seq):
    pos = jnp.arange(seq, dtype=jnp.float32)
    inv = 1.0 / (ROPE_THETA ** (jnp.arange(0, MLA_ROPE_DIM, 2, dtype=jnp.float32) / MLA_ROPE_DIM))
    ang = pos[:, None] * inv[None, :]
    ang = jnp.concatenate([ang, ang], axis=-1)
    return jnp.cos(ang), jnp.sin(ang)


def _apply_rope(x, cos, sin):
    half = x.shape[-1] // 2
    x1, x2 = x[..., :half], x[..., half:]
    rot = jnp.concatenate([-x2, x1], axis=-1)
    return x * cos.astype(x.dtype) + rot * sin.astype(x.dtype)


def _t5_bucket(dist):
    n = jnp.maximum(dist, 0)
    max_exact = NUM_BUCKETS // 2
    nf = jnp.maximum(n, 1).astype(jnp.float32)
    large = max_exact + (jnp.log(nf / max_exact) / math.log(MAX_DISTANCE / max_exact)
                         * (NUM_BUCKETS - max_exact)).astype(jnp.int32)
    large = jnp.minimum(large, NUM_BUCKETS - 1)
    return jnp.where(n < max_exact, n, large)


def _block_sweep(one_block, seq):
    starts = jnp.arange(seq // BLOCK_Q, dtype=jnp.int32) * BLOCK_Q
    out = lax.map(one_block, starts)
    out = jnp.moveaxis(out, 0, 1)
    b, nb, q, h, dv = out.shape
    return out.reshape(b, nb * q, h, dv)


def _diff_attention(q, k, v, lam, bias_table):
    seq = q.shape[1]
    kpos = jnp.arange(seq, dtype=jnp.int32)
    scale = DIFF_QK_DIM ** -0.5
    table = bias_table.astype(jnp.float32)

    def one_block(start):
        qb = lax.dynamic_slice_in_dim(q, start, BLOCK_Q, axis=1)
        s = jnp.einsum('bqhmd,bkhmd->bmhqk', qb, k).astype(jnp.float32) * scale
        dist = (start + jnp.arange(BLOCK_Q, dtype=jnp.int32))[:, None] - kpos[None, :]
        bias = jnp.transpose(table[_t5_bucket(dist)], (2, 0, 1))
        s = jnp.where((dist >= 0)[None, None, None], s + bias[None, None], NEG_INF)
        p = jax.nn.softmax(s, axis=-1)
        w = p[:, 0] - lam * p[:, 1]
        return jnp.einsum('bhqk,bkhd->bqhd', w.astype(v.dtype), v)

    return _block_sweep(one_block, seq)


def _mla_attention(q_nope, q_rope, k_nope, k_rope, v):
    seq = q_nope.shape[1]
    kpos = jnp.arange(seq, dtype=jnp.int32)
    scale = (MLA_NOPE_DIM + MLA_ROPE_DIM) ** -0.5

    def one_block(start):
        qn = lax.dynamic_slice_in_dim(q_nope, start, BLOCK_Q, axis=1)
        qr = lax.dynamic_slice_in_dim(q_rope, start, BLOCK_Q, axis=1)
        s = (jnp.einsum('bqhd,bkhd->bhqk', qn, k_nope)
             + jnp.einsum('bqhr,bkr->bhqk', qr, k_rope)).astype(jnp.float32) * scale
        dist = (start + jnp.arange(BLOCK_Q, dtype=jnp.int32))[:, None] - kpos[None, :]
        s = jnp.where((dist >= 0)[None, None], s, NEG_INF)
        p = jax.nn.softmax(s, axis=-1)
        return jnp.einsum('bhqk,bkhd->bqhd', p.astype(v.dtype), v)

    return _block_sweep(one_block, seq)


def setup_inputs(seed: int = 0) -> dict:
    key = jax.random.key(seed)
    ks = jax.random.split(key, 12)
    f32 = jnp.float32
    x = jax.random.normal(ks[0], (BATCH, SEQ, D_MODEL), f32)
    norm_g = 1.0 + 0.02 * jax.random.normal(ks[1], (DEPTH, D_MODEL), f32)
    w_in = jax.random.normal(ks[2], (DEPTH, D_MODEL, IN_WIDTH), f32) * D_MODEL ** -0.5
    diff_lambda = 0.1 * jax.random.normal(ks[3], (DEPTH, 4, DIFF_QK_DIM), f32)
    diff_subln_g = 1.0 + 0.02 * jax.random.normal(ks[4], (DEPTH, DIFF_V_DIM), f32)
    rel_bias_table = 0.2 * jax.random.normal(ks[5], (NUM_BUCKETS, DIFF_HEADS), f32)
    mla_q_norm_g = 1.0 + 0.02 * jax.random.normal(ks[6], (DEPTH, MLA_Q_RANK), f32)
    w_uq = jax.random.normal(ks[7], (DEPTH, MLA_Q_RANK, MLA_HEADS * (MLA_NOPE_DIM + MLA_ROPE_DIM)), f32) * MLA_Q_RANK ** -0.5
    mla_kv_norm_g = 1.0 + 0.02 * jax.random.normal(ks[8], (DEPTH, MLA_KV_RANK), f32)
    w_ukv = jax.random.normal(ks[9], (DEPTH, MLA_KV_RANK, MLA_HEADS * (MLA_NOPE_DIM + MLA_V_DIM)), f32) * MLA_KV_RANK ** -0.5
    w_out = jax.random.normal(ks[10], (DEPTH, MIX_WIDTH, D_MODEL), f32) * MIX_WIDTH ** -0.5
    final_norm_g = 1.0 + 0.02 * jax.random.normal(ks[11], (D_MODEL,), f32)
    return {"x": x, "norm_g": norm_g, "w_in": w_in, "diff_lambda": diff_lambda,
            "diff_subln_g": diff_subln_g, "rel_bias_table": rel_bias_table,
            "mla_q_norm_g": mla_q_norm_g, "w_uq": w_uq, "mla_kv_norm_g": mla_kv_norm_g,
            "w_ukv": w_ukv, "w_out": w_out, "final_norm_g": final_norm_g}


def reference(x, norm_g, w_in, diff_lambda, diff_subln_g, rel_bias_table,
              mla_q_norm_g, w_uq, mla_kv_norm_g, w_ukv, w_out, final_norm_g):
    b, seq, _ = x.shape
    cos, sin = _rope_tables(seq)
    split_points = [int(v) for v in np.cumsum(IN_SIZES)[:-1]]
    for l in range(DEPTH):
        h = _rms_norm(x, norm_g[l])
        proj = jnp.einsum('bsd,de->bse', h, w_in[l])
        q_a, k_a, v_a, z_a, c_q, c_kv, k_r, z_b = jnp.split(proj, split_points, axis=-1)

        lambda_init = 0.8 - 0.6 * math.exp(-0.3 * l)
        lp = diff_lambda[l].astype(jnp.float32)
        lam = (jnp.exp(jnp.sum(lp[0] * lp[1])) - jnp.exp(jnp.sum(lp[2] * lp[3])) + lambda_init)
        qa = q_a.reshape(b, seq, DIFF_HEADS, 2, DIFF_QK_DIM)
        ka = k_a.reshape(b, seq, DIFF_HEADS, 2, DIFF_QK_DIM)
        va = v_a.reshape(b, seq, DIFF_HEADS, DIFF_V_DIM)
        o_a = _diff_attention(qa, ka, va, lam, rel_bias_table)
        o_a = _rms_norm(o_a, diff_subln_g[l]) * (1.0 - lambda_init)
        o_a = o_a.reshape(b, seq, DIFF_WIDTH) * jax.nn.silu(z_a)

        q_b = jnp.einsum('bsr,re->bse', _rms_norm(c_q, mla_q_norm_g[l]), w_uq[l])
        q_b = q_b.reshape(b, seq, MLA_HEADS, MLA_NOPE_DIM + MLA_ROPE_DIM)
        q_nope, q_rope = q_b[..., :MLA_NOPE_DIM], q_b[..., MLA_NOPE_DIM:]
        q_rope = _apply_rope(q_rope, cos[:, None, :], sin[:, None, :])
        kv = jnp.einsum('bsr,re->bse', _rms_norm(c_kv, mla_kv_norm_g[l]), w_ukv[l])
        kv = kv.reshape(b, seq, MLA_HEADS, MLA_NOPE_DIM + MLA_V_DIM)
        k_nope, v_b = kv[..., :MLA_NOPE_DIM], kv[..., MLA_NOPE_DIM:]
        k_rope = _apply_rope(k_r, cos, sin)
        o_b = _mla_attention(q_nope, q_rope, k_nope, k_rope, v_b)
        o_b = o_b.reshape(b, seq, MLA_WIDTH) * jax.nn.silu(z_b)

        mixed = jnp.concatenate([o_a, o_b], axis=-1)
        x = x + jnp.einsum('bse,ed->bsd', mixed, w_out[l])
    return _rms_norm(x, final_norm_g)
```

```python
import functools
import math

import numpy as np
import jax
import jax.numpy as jnp
from jax import lax
from jax.experimental import pallas as pl
from jax.experimental.pallas import tpu as pltpu

D_MODEL = 2048
DEPTH = 4
DIFF_HEADS = 8
DIFF_QK_DIM = 64
DIFF_V_DIM = 128
DIFF_WIDTH = DIFF_HEADS * DIFF_V_DIM
MLA_HEADS = 8
MLA_Q_RANK = 512
MLA_KV_RANK = 512
MLA_NOPE_DIM = 128
MLA_ROPE_DIM = 64
MLA_V_DIM = 128
MLA_WIDTH = MLA_HEADS * MLA_V_DIM
ROPE_THETA = 10000.0
NUM_BUCKETS = 32
MAX_DISTANCE = 128
EPS = 1e-6
NEG_INF = -1e30

MLA_QK_PAD = 256
MAIN_WIDTH = 5 * 1024
AUX_WIDTH = 512 + 512 + 256
IN_TILE_N = 1280
IN_TILE_M = 512
UP_TILE_M = 512
OUT_TILE_M = 512
ATT_T = 256
VMEM_LIMIT = 56 * 1024 * 1024

F32 = jnp.float32
BF16 = jnp.bfloat16


def _bucket_thresholds():
    n = np.arange(0, MAX_DISTANCE + 1)
    max_exact = NUM_BUCKETS // 2
    nf = np.maximum(n, 1).astype(np.float64)
    large = max_exact + (np.log(nf / max_exact) / math.log(MAX_DISTANCE / max_exact)
                         * (NUM_BUCKETS - max_exact)).astype(np.int64)
    large = np.minimum(large, NUM_BUCKETS - 1)
    bucket = np.where(n < max_exact, n, large)
    return [int(np.argmax(bucket >= b)) for b in range(NUM_BUCKETS)]


_THR = _bucket_thresholds()


def _bias_tile_kernel(table_ref, out_ref):
    h = pl.program_id(0)
    which = pl.program_id(1)
    t = ATT_T
    r = lax.broadcasted_iota(jnp.int32, (t, t), 0)
    c = lax.broadcasted_iota(jnp.int32, (t, t), 1)
    d = r - c + which * t
    far = table_ref[NUM_BUCKETS - 1, h]
    val = jnp.full((t, t), far, F32)
    for b in range(NUM_BUCKETS - 2, -1, -1):
        val = jnp.where(d < _THR[b + 1], table_ref[b, h], val)
    val = val - far
    out_ref[0, 0] = jnp.where(d < 0, NEG_INF, val)


def _bias_tiles(rel_bias_table):
    t = ATT_T
    return pl.pallas_call(
        _bias_tile_kernel,
        grid=(DIFF_HEADS, 2),
        in_specs=[pl.BlockSpec(memory_space=pltpu.SMEM)],
        out_specs=pl.BlockSpec((1, 1, t, t), lambda h, w: (h, w, 0, 0)),
        out_shape=jax.ShapeDtypeStruct((DIFF_HEADS, 2, t, t), F32),
        name="bias_tiles",
    )(rel_bias_table.astype(F32))


def _in_proj_kernel(x_ref, g_ref, w_ref, main_ref, aux_ref, h_sc):
    j = pl.program_id(1)

    @pl.when(j == 0)
    def _():
        x = x_ref[...]
        ms = jnp.mean(x * x, axis=-1, keepdims=True)
        h_sc[...] = (x * lax.rsqrt(ms + EPS) * g_ref[...]).astype(BF16)

    acc = jnp.dot(h_sc[...], w_ref[...], preferred_element_type=F32)

    @pl.when(j < MAIN_WIDTH // IN_TILE_N)
    def _():
        main_ref[...] = acc.astype(BF16)

    @pl.when(j == MAIN_WIDTH // IN_TILE_N)
    def _():
        aux_ref[...] = acc


def _in_proj(x, g, w):
    s = x.shape[0]
    n_main = MAIN_WIDTH // IN_TILE_N
    return pl.pallas_call(
        _in_proj_kernel,
        grid=(s // IN_TILE_M, n_main + 1),
        in_specs=[
            pl.BlockSpec((IN_TILE_M, D_MODEL), lambda i, j: (i, 0)),
            pl.BlockSpec((1, D_MODEL), lambda i, j: (0, 0)),
            pl.BlockSpec((D_MODEL, IN_TILE_N), lambda i, j: (0, j)),
        ],
        out_specs=[
            pl.BlockSpec((IN_TILE_M, IN_TILE_N), lambda i, j: (i, jnp.minimum(j, n_main - 1))),
            pl.BlockSpec((IN_TILE_M, AUX_WIDTH), lambda i, j: (i, 0)),
        ],
        out_shape=[
            jax.ShapeDtypeStruct((s, MAIN_WIDTH), BF16),
            jax.ShapeDtypeStruct((s, AUX_WIDTH), F32),
        ],
        scratch_shapes=[pltpu.VMEM((IN_TILE_M, D_MODEL), BF16)],
        compiler_params=pltpu.CompilerParams(
            dimension_semantics=("arbitrary", "arbitrary"),
            vmem_limit_bytes=VMEM_LIMIT),
        name="in_proj",
    )(x, g, w)


def _rope256(x, cos, sin_a, sin_b):
    return (x * cos
            + pltpu.roll(x, MLA_QK_PAD - MLA_ROPE_DIM // 2, 1) * sin_a
            + pltpu.roll(x, MLA_ROPE_DIM // 2, 1) * sin_b)


def _mla_up_kernel(aux_ref, gq_ref, gkv_ref, wq_ref, wk_ref, wv_ref,
                   cos_ref, sina_ref, sinb_ref, q_ref, k_ref, v_ref):
    scale = (MLA_NOPE_DIM + MLA_ROPE_DIM) ** -0.5
    cos = cos_ref[...]
    sin_a = sina_ref[...]
    sin_b = sinb_ref[...]

    def rms(c, g):
        ms = jnp.mean(c * c, axis=-1, keepdims=True)
        return (c * lax.rsqrt(ms + EPS) * g).astype(BF16)

    cq = rms(aux_ref[:, 0:MLA_Q_RANK], gq_ref[...])
    ckv = rms(aux_ref[:, MLA_Q_RANK:MLA_Q_RANK + MLA_KV_RANK], gkv_ref[...])
    kr = _rope256(aux_ref[:, MLA_Q_RANK + MLA_KV_RANK:], cos, sin_a, sin_b)

    q = jnp.dot(cq, wq_ref[...], preferred_element_type=F32)
    k = jnp.dot(ckv, wk_ref[...], preferred_element_type=F32)
    for h in range(MLA_HEADS):
        sl = slice(h * MLA_QK_PAD, (h + 1) * MLA_QK_PAD)
        q_ref[:, sl] = (_rope256(q[:, sl], cos, sin_a, sin_b) * scale).astype(BF16)
        k_ref[:, sl] = (k[:, sl] + kr).astype(BF16)
    v_ref[...] = jnp.dot(ckv, wv_ref[...], preferred_element_type=F32).astype(BF16)


def _mla_up(aux, gq, gkv, wq, wk, wv, cos, sin_a, sin_b):
    s = aux.shape[0]
    tm = UP_TILE_M
    row = lambda i: (i, 0)
    const = lambda i: (0, 0)
    qk_w = MLA_HEADS * MLA_QK_PAD
    return pl.pallas_call(
        _mla_up_kernel,
        grid=(s // tm,),
        in_specs=[
            pl.BlockSpec((tm, AUX_WIDTH), row),
            pl.BlockSpec((1, MLA_Q_RANK), const),
            pl.BlockSpec((1, MLA_KV_RANK), const),
            pl.BlockSpec((MLA_Q_RANK, qk_w), const),
            pl.BlockSpec((MLA_KV_RANK, qk_w), const),
            pl.BlockSpec((MLA_KV_RANK, MLA_WIDTH), const),
            pl.BlockSpec((tm, MLA_QK_PAD), row),
            pl.BlockSpec((tm, MLA_QK_PAD), row),
            pl.BlockSpec((tm, MLA_QK_PAD), row),
        ],
        out_specs=[
            pl.BlockSpec((tm, qk_w), row),
            pl.BlockSpec((tm, qk_w), row),
            pl.BlockSpec((tm, MLA_WIDTH), row),
        ],
        out_shape=[
            jax.ShapeDtypeStruct((s, qk_w), BF16),
            jax.ShapeDtypeStruct((s, qk_w), BF16),
            jax.ShapeDtypeStruct((s, MLA_WIDTH), BF16),
        ],
        compiler_params=pltpu.CompilerParams(
            dimension_semantics=("arbitrary",),
            vmem_limit_bytes=VMEM_LIMIT),
        name="mla_up",
    )(aux, gq, gkv, wq, wk, wv, cos, sin_a, sin_b)


def _softmax_step(q, kc, vc, add, m_sc, l_sc, acc_sc):
    s = lax.dot_general(q, kc, (((1,), (1,)), ((), ())), preferred_element_type=F32)
    if add is not None:
        s = s + add
    m_prev = m_sc[...]
    m_new = jnp.maximum(m_prev, jnp.max(s, axis=1, keepdims=True))
    alpha = jnp.exp(m_prev - m_new)
    p = jnp.exp(s - m_new)
    l_sc[...] = alpha * l_sc[...] + jnp.sum(p, axis=1, keepdims=True)
    acc_sc[...] = alpha * acc_sc[...] + jnp.dot(p.astype(BF16), vc, preferred_element_type=F32)
    m_sc[...] = m_new


def _silu(z):
    return z * (1.0 / (1.0 + jnp.exp(-z)))


def _diff_attn_kernel(dl_ref, q_ref, k_ref, v_ref, z_ref, bias_ref, g_ref, o_ref,
                      q2_sc, m_sc, l_sc, acc_sc, *, lambda_init):
    t = ATT_T
    i = pl.program_id(1)

    q = q_ref[...]
    lane = lax.broadcasted_iota(jnp.int32, q.shape, 1)
    scale = jnp.asarray(DIFF_QK_DIM ** -0.5, BF16)
    zero = jnp.zeros_like(q)
    q2_sc[0:t, :] = jnp.where(lane < DIFF_QK_DIM, q, zero) * scale
    q2_sc[t:2 * t, :] = jnp.where(lane >= DIFF_QK_DIM, q, zero) * scale
    m_sc[...] = jnp.full(m_sc.shape, NEG_INF, F32)
    l_sc[...] = jnp.zeros(l_sc.shape, F32)
    acc_sc[...] = jnp.zeros(acc_sc.shape, F32)

    def chunk(c, add):
        off = pl.multiple_of(c * t, t)
        _softmax_step(q2_sc[...], k_ref[pl.ds(off, t), :], v_ref[pl.ds(off, t), :],
                      add, m_sc, l_sc, acc_sc)

    def far_body(c, carry):
        chunk(c, None)
        return carry

    lax.fori_loop(0, jnp.maximum(i - 1, 0), far_body, 0)

    def both(b):
        return jnp.concatenate([b, b], axis=0)

    @pl.when(i >= 1)
    def _():
        chunk(i - 1, both(bias_ref[0, 1]))

    chunk(i, both(bias_ref[0, 0]))

    dl = dl_ref[...]
    lam = (jnp.exp(jnp.sum(dl[0:1] * dl[1:2], axis=1, keepdims=True))
           - jnp.exp(jnp.sum(dl[2:3] * dl[3:4], axis=1, keepdims=True)) + lambda_init)
    l = l_sc[...]
    acc = acc_sc[...]
    o = acc[0:t] / l[0:t] - lam * (acc[t:2 * t] / l[t:2 * t])
    ms = jnp.mean(o * o, axis=-1, keepdims=True)
    y = (o * lax.rsqrt(ms + EPS) * g_ref[...]) * (1.0 - lambda_init)
    o_ref[...] = (y * _silu(z_ref[...].astype(F32))).astype(BF16)


def _diff_attn(main, dl, bias, g, lambda_init):
    s = main.shape[0]
    t = ATT_T
    hq = DIFF_WIDTH // 128
    return pl.pallas_call(
        functools.partial(_diff_attn_kernel, lambda_init=lambda_init),
        grid=(DIFF_HEADS, s // t),
        in_specs=[
            pl.BlockSpec((4, DIFF_QK_DIM), lambda h, i: (0, 0)),
            pl.BlockSpec((t, 128), lambda h, i: (i, h)),
            pl.BlockSpec((s, 128), lambda h, i: (0, hq + h)),
            pl.BlockSpec((s, 128), lambda h, i: (0, 2 * hq + h)),
            pl.BlockSpec((t, 128), lambda h, i: (i, 3 * hq + h)),
            pl.BlockSpec((1, 2, t, t), lambda h, i: (h, 0, 0, 0)),
            pl.BlockSpec((1, DIFF_V_DIM), lambda h, i: (0, 0)),
        ],
        out_specs=pl.BlockSpec((t, 128), lambda h, i: (i, h)),
        out_shape=jax.ShapeDtypeStruct((s, DIFF_WIDTH), BF16),
        scratch_shapes=[
            pltpu.VMEM((2 * t, 128), BF16),
            pltpu.VMEM((2 * t, 1), F32),
            pltpu.VMEM((2 * t, 1), F32),
            pltpu.VMEM((2 * t, DIFF_V_DIM), F32),
        ],
        compiler_params=pltpu.CompilerParams(
            dimension_semantics=("arbitrary", "arbitrary"),
            vmem_limit_bytes=VMEM_LIMIT),
        name="diff_attn",
    )(dl, main, main, main, main, bias, g)


def _mla_attn_kernel(q_ref, k_ref, v_ref, z_ref, o_ref, m_sc, l_sc, acc_sc):
    t = ATT_T
    i = pl.program_id(1)
    m_sc[...] = jnp.full(m_sc.shape, NEG_INF, F32)
    l_sc[...] = jnp.zeros(l_sc.shape, F32)
    acc_sc[...] = jnp.zeros(acc_sc.shape, F32)

    def chunk(c, add):
        off = pl.multiple_of(c * t, t)
        _softmax_step(q_ref[...], k_ref[pl.ds(off, t), :], v_ref[pl.ds(off, t), :],
                      add, m_sc, l_sc, acc_sc)

    def far_body(c, carry):
        chunk(c, None)
        return carry

    lax.fori_loop(0, i, far_body, 0)

    r = lax.broadcasted_iota(jnp.int32, (t, t), 0)
    c = lax.broadcasted_iota(jnp.int32, (t, t), 1)
    chunk(i, jnp.where(r >= c, 0.0, NEG_INF).astype(F32))

    o = acc_sc[...] / l_sc[...]
    o_ref[...] = (o * _silu(z_ref[...].astype(F32))).astype(BF16)


def _mla_attn(q, k, v, main):
    s = q.shape[0]
    t = ATT_T
    zb = (MAIN_WIDTH - MLA_WIDTH) // 128
    return pl.pallas_call(
        _mla_attn_kernel,
        grid=(MLA_HEADS, s // t),
        in_specs=[
            pl.BlockSpec((t, MLA_QK_PAD), lambda h, i: (i, h)),
            pl.BlockSpec((s, MLA_QK_PAD), lambda h, i: (0, h)),
            pl.BlockSpec((s, MLA_V_DIM), lambda h, i: (0, h)),
            pl.BlockSpec((t, 128), lambda h, i: (i, zb + h)),
        ],
        out_specs=pl.BlockSpec((t, MLA_V_DIM), lambda h, i: (i, h)),
        out_shape=jax.ShapeDtypeStruct((s, MLA_WIDTH), BF16),
        scratch_shapes=[
            pltpu.VMEM((t, 1), F32),
            pltpu.VMEM((t, 1), F32),
            pltpu.VMEM((t, MLA_V_DIM), F32),
        ],
        compiler_params=pltpu.CompilerParams(
            dimension_semantics=("arbitrary", "arbitrary"),
            vmem_limit_bytes=VMEM_LIMIT),
        name="mla_attn",
    )(q, k, v, main)


def _out_proj_kernel(x_ref, oa_ref, ob_ref, wa_ref, wb_ref, g_ref, y_ref, *, final):
    y = (x_ref[...]
         + jnp.dot(oa_ref[...], wa_ref[...], preferred_element_type=F32)
         + jnp.dot(ob_ref[...], wb_ref[...], preferred_element_type=F32))
    if final:
        ms = jnp.mean(y * y, axis=-1, keepdims=True)
        y = y * lax.rsqrt(ms + EPS) * g_ref[...]
    y_ref[...] = y


def _out_proj(x, oa, ob, wa, wb, g, final):
    s = x.shape[0]
    tm = OUT_TILE_M
    row = lambda i: (i, 0)
    const = lambda i: (0, 0)
    return pl.pallas_call(
        functools.partial(_out_proj_kernel, final=final),
        grid=(s // tm,),
        in_specs=[
            pl.BlockSpec((tm, D_MODEL), row),
            pl.BlockSpec((tm, DIFF_WIDTH), row),
            pl.BlockSpec((tm, MLA_WIDTH), row),
            pl.BlockSpec((DIFF_WIDTH, D_MODEL), const),
            pl.BlockSpec((MLA_WIDTH, D_MODEL), const),
            pl.BlockSpec((1, D_MODEL), const),
        ],
        out_specs=pl.BlockSpec((tm, D_MODEL), row),
        out_shape=jax.ShapeDtypeStruct((s, D_MODEL), F32),
        compiler_params=pltpu.CompilerParams(
            dimension_semantics=("arbitrary",),
            vmem_limit_bytes=VMEM_LIMIT),
        name="out_proj",
    )(x, oa, ob, wa, wb, g)


def _prep_w_in(w_in):
    d = w_in.shape[1]
    a = w_in[:, :, 0:4096]
    c = w_in[:, :, 4096:5120]
    kr = w_in[:, :, 5120:5184]
    zb = w_in[:, :, 5184:6208]
    pad = lambda n: jnp.zeros((DEPTH, d, n), w_in.dtype)
    return jnp.concatenate(
        [a, zb, c, pad(MLA_NOPE_DIM), kr, pad(MLA_QK_PAD - MLA_NOPE_DIM - MLA_ROPE_DIM)],
        axis=-1).astype(BF16)


def _prep_w_uq(w_uq):
    w = w_uq.reshape(DEPTH, MLA_Q_RANK, MLA_HEADS, MLA_NOPE_DIM + MLA_ROPE_DIM)
    w = jnp.pad(w, ((0, 0), (0, 0), (0, 0), (0, MLA_QK_PAD - MLA_NOPE_DIM - MLA_ROPE_DIM)))
    return w.reshape(DEPTH, MLA_Q_RANK, MLA_HEADS * MLA_QK_PAD).astype(BF16)


def _prep_w_ukv(w_ukv):
    w = w_ukv.reshape(DEPTH, MLA_KV_RANK, MLA_HEADS, MLA_NOPE_DIM + MLA_V_DIM)
    wk = jnp.pad(w[..., :MLA_NOPE_DIM], ((0, 0), (0, 0), (0, 0), (0, MLA_QK_PAD - MLA_NOPE_DIM)))
    wv = w[..., MLA_NOPE_DIM:]
    return (wk.reshape(DEPTH, MLA_KV_RANK, MLA_HEADS * MLA_QK_PAD).astype(BF16),
            wv.reshape(DEPTH, MLA_KV_RANK, MLA_WIDTH).astype(BF16))


def _rope_tables256(seq):
    pos = jnp.arange(seq, dtype=F32)
    inv = 1.0 / (ROPE_THETA ** (jnp.arange(0, MLA_ROPE_DIM, 2, dtype=F32) / MLA_ROPE_DIM))
    ang = pos[:, None] * inv[None, :]
    cos, sin = jnp.cos(ang), jnp.sin(ang)
    half = MLA_ROPE_DIM // 2
    tail = MLA_QK_PAD - MLA_NOPE_DIM - MLA_ROPE_DIM
    ones = jnp.ones((seq, MLA_NOPE_DIM), F32)
    z = lambda n: jnp.zeros((seq, n), F32)
    cos256 = jnp.concatenate([ones, cos, cos, z(tail)], axis=1)
    sin_a = jnp.concatenate([z(MLA_NOPE_DIM), -sin, z(half), z(tail)], axis=1)
    sin_b = jnp.concatenate([z(MLA_NOPE_DIM), z(half), sin, z(tail)], axis=1)
    return cos256, sin_a, sin_b


def kernel(x, norm_g, w_in, diff_lambda, diff_subln_g, rel_bias_table, mla_q_norm_g,
           w_uq, mla_kv_norm_g, w_ukv, w_out, final_norm_g):
    b, seq, _ = x.shape
    assert b == 1 and seq % ATT_T == 0 and seq % IN_TILE_M == 0
    xs = x.reshape(seq, D_MODEL).astype(F32)

    w_in_p = _prep_w_in(w_in)
    w_uq_p = _prep_w_uq(w_uq)
    w_k_p, w_v_p = _prep_w_ukv(w_ukv)
    w_out_p = w_out.astype(BF16)
    cos256, sin_a, sin_b = _rope_tables256(seq)
    bias = _bias_tiles(rel_bias_table)
    fg = final_norm_g.reshape(1, D_MODEL).astype(F32)

    for l in range(DEPTH):
        lambda_init = 0.8 - 0.6 * math.exp(-0.3 * l)
        main, aux = _in_proj(xs, norm_g[l].reshape(1, D_MODEL).astype(F32), w_in_p[l])
        o_a = _diff_attn(main, diff_lambda[l].astype(F32), bias,
                         diff_subln_g[l].reshape(1, DIFF_V_DIM).astype(F32), lambda_init)
        q_b, k_b, v_b = _mla_up(aux, mla_q_norm_g[l].reshape(1, MLA_Q_RANK).astype(F32),
                                mla_kv_norm_g[l].reshape(1, MLA_KV_RANK).astype(F32),
                                w_uq_p[l], w_k_p[l], w_v_p[l], cos256, sin_a, sin_b)
        o_b = _mla_attn(q_b, k_b, v_b, main)
        xs = _out_proj(xs, o_a, o_b, w_out_p[l, :DIFF_WIDTH], w_out_p[l, DIFF_WIDTH:], fg,
                       final=(l == DEPTH - 1))
    return xs.reshape(b, seq, D_MODEL)
```
